```python
import math
import jax, jax.numpy as jnp
from jax import lax
import numpy as np

D_MODEL = 1024
BATCH = 2
SEQ = 8192
DEPTH = 2

N_MEM = 256
D_MIX = D_MODEL
D_DIFF = D_MIX // 2
N_DIFF_HEADS = 4
DIFF_V_DIM = D_DIFF // N_DIFF_HEADS
DIFF_QK_DIM = DIFF_V_DIM // 2
ROPE_DIM = DIFF_QK_DIM // 4
ROPE_THETA = 500000.0
D_GMLP = D_MIX - D_DIFF
N_GMLP_GROUPS = 4
GMLP_GROUP_DIM = D_GMLP // N_GMLP_GROUPS
CHUNK = 128
Q_BLOCK = 128
N_CROSS_HEADS = 4
CROSS_HEAD_DIM = D_MODEL // N_CROSS_HEADS
D_FF = ((8 * D_MODEL // 3 + 255) // 256) * 256
D_IN = 3 * D_DIFF + 2 * D_GMLP
NORM_EPS = 1e-6
NEG_INF = -1e30

kernel_name = 'hybrid_diffattn_gmlp_macaron'


def rms_norm(x, g):
    xf = x.astype(jnp.float32)
    y = xf * lax.rsqrt(jnp.mean(xf * xf, axis=-1, keepdims=True) + NORM_EPS)
    return (y * g.astype(jnp.float32)).astype(x.dtype)


def layer_norm(x, g, b):
    xf = x.astype(jnp.float32)
    mu = jnp.mean(xf, axis=-1, keepdims=True)
    xc = xf - mu
    y = xc * lax.rsqrt(jnp.mean(xc * xc, axis=-1, keepdims=True) + NORM_EPS)
    return (y * g.astype(jnp.float32) + b.astype(jnp.float32)).astype(x.dtype)


def swiglu(x, wgu, wd):
    gate, up = jnp.split(x @ wgu, 2, axis=-1)
    return (jax.nn.silu(gate) * up) @ wd


def rope_tables(positions):
    inv_freq = 1.0 / (ROPE_THETA ** (jnp.arange(0, ROPE_DIM, 2, dtype=jnp.float32) / ROPE_DIM))
    ang = positions.astype(jnp.float32)[..., None] * inv_freq
    return jnp.cos(ang), jnp.sin(ang)


def partial_rope(t, cos, sin):
    c = cos[:, :, None, None, :].astype(t.dtype)
    s = sin[:, :, None, None, :].astype(t.dtype)
    half = ROPE_DIM // 2
    t1 = t[..., :half]
    t2 = t[..., half:ROPE_DIM]
    return jnp.concatenate([t1 * c - t2 * s, t2 * c + t1 * s, t[..., ROPE_DIM:]], axis=-1)


def diff_attention(q, k, v, lam):
    b, s = q.shape[0], q.shape[1]
    nb = s // Q_BLOCK
    k1 = k[..., 0, :].transpose(0, 2, 1, 3)
    k2 = k[..., 1, :].transpose(0, 2, 1, 3)
    vh = v.transpose(0, 2, 1, 3)
    qb = (q * (DIFF_QK_DIM ** -0.5)).reshape(b, nb, Q_BLOCK, N_DIFF_HEADS, 2, DIFF_QK_DIM)
    qb = qb.transpose(1, 0, 3, 4, 2, 5)
    key_pos = jnp.arange(s)
    lam32 = lam.astype(jnp.float32)

    def one_block(args):
        qblk, start = args
        q_pos = start + jnp.arange(Q_BLOCK)
        causal = key_pos[None, :] <= q_pos[:, None]
        s1 = jnp.einsum('bhqd,bhkd->bhqk', qblk[:, :, 0], k1).astype(jnp.float32)
        s2 = jnp.einsum('bhqd,bhkd->bhqk', qblk[:, :, 1], k2).astype(jnp.float32)
        p1 = jax.nn.softmax(jnp.where(causal, s1, NEG_INF), axis=-1)
        p2 = jax.nn.softmax(jnp.where(causal, s2, NEG_INF), axis=-1)
        w = (p1 - lam32 * p2).astype(vh.dtype)
        return jnp.einsum('bhqk,bhkd->bhqd', w, vh)

    out = lax.map(one_block, (qb, jnp.arange(nb) * Q_BLOCK))
    return out.transpose(1, 0, 3, 2, 4).reshape(b, s, N_DIFF_HEADS, DIFF_V_DIM)


def chunked_spatial_gating(u, v, ws, bs, ln_g, ln_b):
    b, s = u.shape[0], u.shape[1]
    vn = layer_norm(v, ln_g, ln_b)
    vc = vn.reshape(b, s // CHUNK, CHUNK, N_GMLP_GROUPS, GMLP_GROUP_DIM)
    causal = jnp.tril(jnp.ones((CHUNK, CHUNK), dtype=bool))
    w = jnp.where(causal[None], ws, 0)
    mixed = jnp.einsum('gts,bnsgc->bntgc', w, vc) + bs.T[None, None, :, :, None]
    return (u * mixed.reshape(b, s, N_GMLP_GROUPS, GMLP_GROUP_DIM)).reshape(b, s, D_GMLP)


def cross_attention(hn, mem_n, wq, wkv, wo):
    b, s = hn.shape[0], hn.shape[1]
    m = mem_n.shape[1]
    q = (hn @ wq).reshape(b, s, N_CROSS_HEADS, CROSS_HEAD_DIM)
    kv = (mem_n @ wkv).reshape(b, m, 2, N_CROSS_HEADS, CROSS_HEAD_DIM)
    sc = jnp.einsum('bshd,bmhd->bhsm', q, kv[:, :, 0]).astype(jnp.float32) * (CROSS_HEAD_DIM ** -0.5)
    p = jax.nn.softmax(sc, axis=-1).astype(hn.dtype)
    o = jnp.einsum('bhsm,bmhd->bshd', p, kv[:, :, 1]).reshape(b, s, D_MODEL)
    return o @ wo


def setup_inputs(seed: int = 0) -> dict:
    key = jax.random.key(seed)
    ks = jax.random.split(key, 32)
    f32 = jnp.float32

    def nrm(k, shape, fan_in):
        return jax.random.normal(k, shape, f32) * (fan_in ** -0.5)

    def gain(k, shape):
        return 1.0 + 0.02 * jax.random.normal(k, shape, f32)

    L = DEPTH
    return {
        'x': jax.random.normal(ks[0], (BATCH, SEQ, D_MODEL), f32),
        'mem': jax.random.normal(ks[1], (BATCH, N_MEM, D_MODEL), f32),
        'positions': jnp.broadcast_to(jnp.arange(SEQ, dtype=jnp.int32)[None, :], (BATCH, SEQ)),
        'ffn1_norm': gain(ks[2], (L, D_MODEL)),
        'ffn1_wgu': nrm(ks[3], (L, D_MODEL, 2 * D_FF), D_MODEL),
        'ffn1_wd': nrm(ks[4], (L, D_FF, D_MODEL), D_FF),
        'mix_norm': gain(ks[5], (L, D_MODEL)),
        'w_in': nrm(ks[6], (L, D_MODEL, D_IN), D_MODEL),
        'lam_q1': 0.1 * jax.random.normal(ks[7], (L, DIFF_QK_DIM), f32),
        'lam_k1': 0.1 * jax.random.normal(ks[8], (L, DIFF_QK_DIM), f32),
        'lam_q2': 0.1 * jax.random.normal(ks[9], (L, DIFF_QK_DIM), f32),
        'lam_k2': 0.1 * jax.random.normal(ks[10], (L, DIFF_QK_DIM), f32),
        'subln': gain(ks[11], (L, DIFF_V_DIM)),
        'gmlp_ln_g': gain(ks[12], (L, N_GMLP_GROUPS, GMLP_GROUP_DIM)),
        'gmlp_ln_b': 0.02 * jax.random.normal(ks[13], (L, N_GMLP_GROUPS, GMLP_GROUP_DIM), f32),
        'gmlp_ws': nrm(ks[14], (L, N_GMLP_GROUPS, CHUNK, CHUNK), CHUNK),
        'gmlp_bs': gain(ks[15], (L, N_GMLP_GROUPS, CHUNK)),
        'w_out': nrm(ks[16], (L, D_MIX, D_MODEL), D_MIX),
        'cross_norm': gain(ks[17], (L, D_MODEL)),
        'mem_norm': gain(ks[18], (L, D_MODEL)),
        'cross_wq': nrm(ks[19], (L, D_MODEL, D_MODEL), D_MODEL),
        'cross_wkv': nrm(ks[20], (L, D_MODEL, 2 * D_MODEL), D_MODEL),
        'cross_wo': nrm(ks[21], (L, D_MODEL, D_MODEL), D_MODEL),
        'ffn2_norm': gain(ks[22], (L, D_MODEL)),
        'ffn2_wgu': nrm(ks[23], (L, D_MODEL, 2 * D_FF), D_MODEL),
        'ffn2_wd': nrm(ks[24], (L, D_FF, D_MODEL), D_FF),
        'final_norm': gain(ks[25], (D_MODEL,)),
    }


def reference(x, mem, positions, ffn1_norm, ffn1_wgu, ffn1_wd, mix_norm, w_in,
              lam_q1, lam_k1, lam_q2, lam_k2, subln, gmlp_ln_g, gmlp_ln_b, gmlp_ws,
              gmlp_bs, w_out, cross_norm, mem_norm, cross_wq, cross_wkv, cross_wo,
              ffn2_norm, ffn2_wgu, ffn2_wd, final_norm):
    b, s = x.shape[0], x.shape[1]
    cos, sin = rope_tables(positions)
    h = x
    for i in range(DEPTH):
        h = h + 0.5 * swiglu(rms_norm(h, ffn1_norm[i]), ffn1_wgu[i], ffn1_wd[i])

        hn = rms_norm(h, mix_norm[i])
        proj = hn @ w_in[i]
        q, k, v, gu, gv = jnp.split(
            proj, [D_DIFF, 2 * D_DIFF, 3 * D_DIFF, 3 * D_DIFF + D_GMLP], axis=-1)
        q = partial_rope(q.reshape(b, s, N_DIFF_HEADS, 2, DIFF_QK_DIM), cos, sin)
        k = partial_rope(k.reshape(b, s, N_DIFF_HEADS, 2, DIFF_QK_DIM), cos, sin)
        v = v.reshape(b, s, N_DIFF_HEADS, DIFF_V_DIM)
        lam_init = 0.8 - 0.6 * math.exp(-0.3 * i)
        lam = (jnp.exp(jnp.sum(lam_q1[i].astype(jnp.float32) * lam_k1[i].astype(jnp.float32)))
               - jnp.exp(jnp.sum(lam_q2[i].astype(jnp.float32) * lam_k2[i].astype(jnp.float32)))
               + lam_init)
        attn = diff_attention(q, k, v, lam)
        attn = (rms_norm(attn, subln[i]) * (1.0 - lam_init)).reshape(b, s, D_DIFF)

        gu = jax.nn.gelu(gu, approximate=False).reshape(b, s, N_GMLP_GROUPS, GMLP_GROUP_DIM)
        gv = jax.nn.gelu(gv, approximate=False).reshape(b, s, N_GMLP_GROUPS, GMLP_GROUP_DIM)
        gm = chunked_spatial_gating(gu, gv, gmlp_ws[i], gmlp_bs[i], gmlp_ln_g[i], gmlp_ln_b[i])

        h = h + jnp.concatenate([attn, gm], axis=-1) @ w_out[i]

        h = h + cross_attention(rms_norm(h, cross_norm[i]), rms_norm(mem, mem_norm[i]),
                                cross_wq[i], cross_wkv[i], cross_wo[i])

        h = h + 0.5 * swiglu(rms_norm(h, ffn2_norm[i]), ffn2_wgu[i], ffn2_wd[i])
    return rms_norm(h, final_norm)
```

```python
import functools
import math

import jax
import jax.numpy as jnp
from jax import lax
from jax.experimental import pallas as pl
from jax.experimental.pallas import tpu as pltpu

F32 = jnp.float32
BF16 = jnp.bfloat16

D_MODEL = 1024
DEPTH = 2
N_MEM = 256
D_DIFF = 512
N_DIFF_HEADS = 4
DIFF_V_DIM = 128
DIFF_QK_DIM = 64
ROPE_DIM = 16
ROPE_THETA = 500000.0
D_GMLP = 512
N_GMLP_GROUPS = 4
GMLP_GROUP_DIM = 128
CHUNK = 128
N_CROSS_HEADS = 4
CROSS_HEAD_DIM = 256
D_FF = 2816
D_IN = 3 * D_DIFF + 2 * D_GMLP
NORM_EPS = 1e-6
NEG_INF = -1e30

LANES = 128
MXU_N = 256
VMEM_LIMIT = 56 * 1024 * 1024

TM = 512
TQ = 512
ROPE_TM = 1024

NT_DIMS = (((1,), (1,)), ((), ()))


def _rms(x, g):
    return x * lax.rsqrt(jnp.mean(x * x, axis=-1, keepdims=True) + NORM_EPS) * g


def _dot(a, b):
    return jnp.dot(a, b, preferred_element_type=F32)


def _resident(shape):
    return pl.BlockSpec(shape, lambda *_: (0,) * len(shape), pipeline_mode=pl.Buffered(1))


def _params(n_axes):
    return pltpu.CompilerParams(
        dimension_semantics=("parallel",) * n_axes, vmem_limit_bytes=VMEM_LIMIT)


def _rope_kernel(pos_ref, freq_ref, ma_ref, mb_ref, cos_ref, sa_ref, sb_ref):
    ang = pos_ref[...].astype(F32) * freq_ref[...]
    cos_ref[...] = jnp.cos(ang)
    s = jnp.sin(ang)
    sa_ref[...] = s * ma_ref[...]
    sb_ref[...] = s * mb_ref[...]


def _rope_tables(positions):
    t = positions.size
    half = ROPE_DIM // 2
    inv_freq = 1.0 / (ROPE_THETA ** (jnp.arange(0, ROPE_DIM, 2, dtype=F32) / ROPE_DIM))
    r = jnp.arange(LANES) % DIFF_QK_DIM
    freq = jnp.where(r < ROPE_DIM, inv_freq[r % half], 0.0).astype(F32)[None, :]
    ma = jnp.where(r < half, -1.0, 0.0).astype(F32)[None, :]
    mb = jnp.where((r >= half) & (r < ROPE_DIM), 1.0, 0.0).astype(F32)[None, :]
    vec = pl.BlockSpec((1, LANES), lambda i: (0, 0))
    tab = pl.BlockSpec((ROPE_TM, LANES), lambda i: (i, 0))
    out = jax.ShapeDtypeStruct((t, LANES), F32)
    return pl.pallas_call(
        _rope_kernel,
        grid=(t // ROPE_TM,),
        in_specs=[pl.BlockSpec((ROPE_TM, 1), lambda i: (i, 0)), vec, vec, vec],
        out_specs=[tab, tab, tab],
        out_shape=[out, out, out],
        compiler_params=_params(1),
        name="rope_tables",
    )(positions.reshape(t, 1), freq, ma, mb)


FF_CHUNK = MXU_N


def _ffn_kernel(*refs, final):
    if final:
        x_ref, g_ref, wgu_ref, wd_ref, fn_ref, o_ref, a_ref = refs
    else:
        x_ref, g_ref, wgu_ref, wd_ref, o_ref, a_ref = refs
    x = x_ref[...]
    xn = _rms(x, g_ref[...]).astype(BF16)
    for c in range(D_FF // FF_CHUNK):
        lo = c * FF_CHUNK
        gate = _dot(xn, wgu_ref[:, lo:lo + FF_CHUNK])
        up = _dot(xn, wgu_ref[:, D_FF + lo:D_FF + lo + FF_CHUNK])
        a_ref[:, lo:lo + FF_CHUNK] = (gate * jax.nn.sigmoid(gate) * up).astype(BF16)
    y = x + 0.5 * _dot(a_ref[...], wd_ref[...])
    if final:
        y = _rms(y, fn_ref[...])
    o_ref[...] = y


def _ffn(h, norm_g, wgu, wd, final_g=None):
    t = h.shape[0]
    final = final_g is not None
    row = pl.BlockSpec((TM, D_MODEL), lambda i: (i, 0))
    in_specs = [row, _resident((1, D_MODEL)), _resident((D_MODEL, 2 * D_FF)),
                _resident((D_FF, D_MODEL))]
    args = [h, norm_g[None, :], wgu, wd]
    if final:
        in_specs.append(_resident((1, D_MODEL)))
        args.append(final_g[None, :])
    return pl.pallas_call(
        functools.partial(_ffn_kernel, final=final),
        grid=(t // TM,),
        in_specs=in_specs,
        out_specs=row,
        out_shape=jax.ShapeDtypeStruct((t, D_MODEL), F32),
        scratch_shapes=[pltpu.VMEM((TM, D_FF), BF16)],
        compiler_params=_params(1),
        name="ffn",
    )(*args)


Q_SCALE = DIFF_QK_DIM ** -0.5
SQRT_HALF = math.sqrt(0.5)


def _gelu(x):
    return 0.5 * x * (1.0 + lax.erf(x * SQRT_HALF))


def _mix_kernel(h_ref, g_ref, win_ref, cos_ref, sa_ref, sb_ref, lng_ref, lnb_ref,
                ws_ref, bias_ref, qkv_ref, gm_ref):
    hn = _rms(h_ref[...], g_ref[...]).astype(BF16)
    cos = cos_ref[...]
    sa = sa_ref[...]
    sb = sb_ref[...]

    for part, scale in ((0, Q_SCALE), (1, 1.0)):
        proj = _dot(hn, win_ref[:, part * D_DIFF:(part + 1) * D_DIFF])
        for hd in range(N_DIFF_HEADS):
            t = proj[:, hd * LANES:(hd + 1) * LANES]
            r = (t * cos + pltpu.roll(t, LANES - ROPE_DIM // 2, 1) * sa
                 + pltpu.roll(t, ROPE_DIM // 2, 1) * sb)
            lo = part * D_DIFF + hd * LANES
            qkv_ref[:, lo:lo + LANES] = (r * scale).astype(BF16)
    qkv_ref[:, 2 * D_DIFF:] = _dot(hn, win_ref[:, 2 * D_DIFF:3 * D_DIFF]).astype(BF16)

    gu = _gelu(_dot(hn, win_ref[:, 3 * D_DIFF:3 * D_DIFF + D_GMLP]))
    gv = _gelu(_dot(hn, win_ref[:, 3 * D_DIFF + D_GMLP:]))
    tril = (lax.broadcasted_iota(jnp.int32, (CHUNK, CHUNK), 1)
            <= lax.broadcasted_iota(jnp.int32, (CHUNK, CHUNK), 0))
    for g in range(N_GMLP_GROUPS):
        cols = slice(g * GMLP_GROUP_DIM, (g + 1) * GMLP_GROUP_DIM)
        v = gv[:, cols]
        xc = v - jnp.mean(v, axis=-1, keepdims=True)
        vn = xc * lax.rsqrt(jnp.mean(xc * xc, axis=-1, keepdims=True) + NORM_EPS)
        vn = (vn * lng_ref[:, cols] + lnb_ref[:, cols]).astype(BF16)
        w = jnp.where(tril, ws_ref[g], 0.0).astype(BF16)
        for c in range(TM // CHUNK):
            rows = slice(c * CHUNK, (c + 1) * CHUNK)
            mixed = _dot(w, vn[rows, :]) + bias_ref[g]
            gm_ref[rows, cols] = (gu[rows, cols] * mixed).astype(BF16)


def _mix(h, norm_g, w_in, cos, sa, sb, ln_g, ln_b, ws, bs):
    t = h.shape[0]
    bias = jnp.broadcast_to(bs[:, :, None], (N_GMLP_GROUPS, CHUNK, GMLP_GROUP_DIM))
    tab = pl.BlockSpec((TM, LANES), lambda i: (i, 0))
    return pl.pallas_call(
        _mix_kernel,
        grid=(t // TM,),
        in_specs=[pl.BlockSpec((TM, D_MODEL), lambda i: (i, 0)),
                  _resident((1, D_MODEL)), _resident((D_MODEL, D_IN)),
                  tab, tab, tab,
                  _resident((1, D_GMLP)), _resident((1, D_GMLP)),
                  _resident((N_GMLP_GROUPS, CHUNK, CHUNK)),
                  _resident((N_GMLP_GROUPS, CHUNK, GMLP_GROUP_DIM))],
        out_specs=[pl.BlockSpec((TM, 3 * D_DIFF), lambda i: (i, 0)),
                   pl.BlockSpec((TM, D_GMLP), lambda i: (i, 0))],
        out_shape=[jax.ShapeDtypeStruct((t, 3 * D_DIFF), BF16),
                   jax.ShapeDtypeStruct((t, D_GMLP), BF16)],
        compiler_params=_params(1),
        name="mix_proj",
    )(h, norm_g[None, :], w_in, cos, sa, sb, ln_g.reshape(1, D_GMLP),
      ln_b.reshape(1, D_GMLP), ws, bias)


def _attn_kernel(q_ref, k_ref, v_ref, lam_ref, sub_ref, o_ref,
                 m1_ref, l1_ref, a1_ref, m2_ref, l2_ref, a2_ref, *, lam_init):
    qi = pl.program_id(2)
    q = q_ref[0]
    lane = lax.broadcasted_iota(jnp.int32, q.shape, 1)
    zero = jnp.zeros_like(q)
    q_maps = (jnp.where(lane < DIFF_QK_DIM, q, zero), jnp.where(lane >= DIFF_QK_DIM, q, zero))
    stats = ((m1_ref, l1_ref, a1_ref), (m2_ref, l2_ref, a2_ref))

    start = pl.multiple_of(qi * TQ, TQ)
    kd = k_ref[0, pl.ds(start, TQ), :]
    vd = v_ref[0, pl.ds(start, TQ), :]
    causal = (lax.broadcasted_iota(jnp.int32, (TQ, TQ), 1)
              <= lax.broadcasted_iota(jnp.int32, (TQ, TQ), 0))
    for qm, (m_ref, l_ref, a_ref) in zip(q_maps, stats):
        s = lax.dot_general(qm, kd, NT_DIMS, preferred_element_type=F32)
        s = jnp.where(causal, s, NEG_INF)
        m = jnp.max(s, axis=-1, keepdims=True)
        p = jnp.exp(s - m)
        m_ref[...] = m
        l_ref[...] = jnp.sum(p, axis=-1, keepdims=True)
        a_ref[...] = _dot(p.astype(BF16), vd)

    def body(j, carry):
        off = pl.multiple_of(j * TQ, TQ)
        kb = k_ref[0, pl.ds(off, TQ), :]
        vb = v_ref[0, pl.ds(off, TQ), :]
        for qm, (m_ref, l_ref, a_ref) in zip(q_maps, stats):
            s = lax.dot_general(qm, kb, NT_DIMS, preferred_element_type=F32)
            m_old = m_ref[...]
            m_new = jnp.maximum(m_old, jnp.max(s, axis=-1, keepdims=True))
            alpha = jnp.exp(m_old - m_new)
            p = jnp.exp(s - m_new)
            l_ref[...] = alpha * l_ref[...] + jnp.sum(p, axis=-1, keepdims=True)
            a_ref[...] = alpha * a_ref[...] + _dot(p.astype(BF16), vb)
            m_ref[...] = m_new
        return carry

    lax.fori_loop(0, qi, body, 0)

    lv = lam_ref[...]
    lam = (jnp.exp(jnp.sum(lv[0:1] * lv[1:2], axis=-1, keepdims=True))
           - jnp.exp(jnp.sum(lv[2:3] * lv[3:4], axis=-1, keepdims=True)) + lam_init)
    o = a1_ref[...] / l1_ref[...] - lam * (a2_ref[...] / l2_ref[...])
    o_ref[0] = (_rms(o, sub_ref[...]) * (1.0 - lam_init)).astype(BF16)


def _diff_attention(qkv, lam_vecs, subln, lam_init, batch, seq):
    h = N_DIFF_HEADS
    kv_spec = lambda part: pl.BlockSpec(
        (1, seq, LANES), lambda b, hd, qi: (b, 0, part * h + hd))
    return pl.pallas_call(
        functools.partial(_attn_kernel, lam_init=lam_init),
        grid=(batch, h, seq // TQ),
        in_specs=[pl.BlockSpec((1, TQ, LANES), lambda b, hd, qi: (b, qi, hd)),
                  kv_spec(1), kv_spec(2),
                  pl.BlockSpec((4, DIFF_QK_DIM), lambda b, hd, qi: (0, 0)),
                  pl.BlockSpec((1, DIFF_V_DIM), lambda b, hd, qi: (0, 0))],
        out_specs=pl.BlockSpec((1, TQ, LANES), lambda b, hd, qi: (b, qi, hd)),
        out_shape=jax.ShapeDtypeStruct((batch, seq, D_DIFF), BF16),
        scratch_shapes=[pltpu.VMEM((TQ, 1), F32), pltpu.VMEM((TQ, 1), F32),
                        pltpu.VMEM((TQ, DIFF_V_DIM), F32)] * 2,
        compiler_params=_params(3),
        name="diff_attn",
    )(qkv, qkv, qkv, lam_vecs, subln[None, :])


def _memkv_kernel(mem_ref, g_ref, wkv_ref, k_ref, v_ref):
    mn = _rms(mem_ref[...], g_ref[...]).astype(BF16)
    kv = _dot(mn, wkv_ref[...])
    k_ref[...] = kv[:, :D_MODEL].astype(BF16)
    v_ref[...] = kv[:, D_MODEL:].astype(BF16)


def _mem_kv(mem2d, norm_g, wkv):
    t = mem2d.shape[0]
    row = pl.BlockSpec((N_MEM, D_MODEL), lambda i: (i, 0))
    out = jax.ShapeDtypeStruct((t, D_MODEL), BF16)
    return pl.pallas_call(
        _memkv_kernel,
        grid=(t // N_MEM,),
        in_specs=[row, _resident((1, D_MODEL)), _resident((D_MODEL, 2 * D_MODEL))],
        out_specs=[row, row],
        out_shape=[out, out],
        compiler_params=_params(1),
        name="mem_kv",
    )(mem2d, norm_g[None, :], wkv)


CROSS_SCALE = CROSS_HEAD_DIM ** -0.5


def _outcross_kernel(h_ref, attn_ref, gm_ref, woa_ref, wog_ref, cn_ref, wq_ref,
                     k_ref, v_ref, wo_ref, o_ref, ctx_ref):
    h2 = h_ref[...] + _dot(attn_ref[...], woa_ref[...]) + _dot(gm_ref[...], wog_ref[...])
    hn = _rms(h2, cn_ref[...]).astype(BF16)
    q = (_dot(hn, wq_ref[...]) * CROSS_SCALE).astype(BF16)
    for hd in range(N_CROSS_HEADS):
        cols = slice(hd * CROSS_HEAD_DIM, (hd + 1) * CROSS_HEAD_DIM)
        sc = lax.dot_general(q[:, cols], k_ref[:, cols], NT_DIMS, preferred_element_type=F32)
        p = jnp.exp(sc - jnp.max(sc, axis=-1, keepdims=True))
        p = p / jnp.sum(p, axis=-1, keepdims=True)
        ctx_ref[:, cols] = _dot(p.astype(BF16), v_ref[:, cols]).astype(BF16)
    o_ref[...] = h2 + _dot(ctx_ref[...], wo_ref[...])


def _out_cross(h, attn, gm, w_out, norm_g, wq, kmem, vmem, wo, seq):
    t = h.shape[0]
    row = pl.BlockSpec((TM, D_MODEL), lambda i: (i, 0))
    half = pl.BlockSpec((TM, D_DIFF), lambda i: (i, 0))
    mem = pl.BlockSpec((N_MEM, D_MODEL), lambda i: (i * TM // seq, 0))
    return pl.pallas_call(
        _outcross_kernel,
        grid=(t // TM,),
        in_specs=[row, half, half,
                  _resident((D_DIFF, D_MODEL)), _resident((D_GMLP, D_MODEL)),
                  _resident((1, D_MODEL)), _resident((D_MODEL, D_MODEL)),
                  mem, mem, _resident((D_MODEL, D_MODEL))],
        out_specs=row,
        out_shape=jax.ShapeDtypeStruct((t, D_MODEL), F32),
        scratch_shapes=[pltpu.VMEM((TM, D_MODEL), BF16)],
        compiler_params=_params(1),
        name="out_cross",
    )(h, attn, gm, w_out[:D_DIFF], w_out[D_DIFF:], norm_g[None, :], wq, kmem, vmem, wo)


def kernel(x, mem, positions, ffn1_norm, ffn1_wgu, ffn1_wd, mix_norm, w_in, lam_q1, lam_k1,
           lam_q2, lam_k2, subln, gmlp_ln_g, gmlp_ln_b, gmlp_ws, gmlp_bs, w_out, cross_norm,
           mem_norm, cross_wq, cross_wkv, cross_wo, ffn2_norm, ffn2_wgu, ffn2_wd, final_norm):
    batch, seq, _ = x.shape
    t = batch * seq
    assert seq % TQ == 0 and seq % TM == 0 and t % ROPE_TM == 0
    cos, sa, sb = _rope_tables(positions)
    h = x.reshape(t, D_MODEL)
    mem2d = mem.reshape(batch * N_MEM, D_MODEL)
    bf = lambda w: w.astype(BF16)
    for i in range(DEPTH):
        lam_init = 0.8 - 0.6 * math.exp(-0.3 * i)
        h = _ffn(h, ffn1_norm[i], bf(ffn1_wgu[i]), bf(ffn1_wd[i]))
        qkv, gm = _mix(h, mix_norm[i], bf(w_in[i]), cos, sa, sb, gmlp_ln_g[i], gmlp_ln_b[i],
                       gmlp_ws[i], gmlp_bs[i])
        lam_vecs = jnp.stack([lam_q1[i], lam_k1[i], lam_q2[i], lam_k2[i]])
        attn = _diff_attention(qkv.reshape(batch, seq, 3 * D_DIFF), lam_vecs, subln[i],
                               lam_init, batch, seq)
        kmem, vmem = _mem_kv(mem2d, mem_norm[i], bf(cross_wkv[i]))
        h = _out_cross(h, attn.reshape(t, D_DIFF), gm, bf(w_out[i]), cross_norm[i],
                       bf(cross_wq[i]), kmem, vmem, bf(cross_wo[i]), seq)
        h = _ffn(h, ffn2_norm[i], bf(ffn2_wgu[i]), bf(ffn2_wd[i]),
                 final_g=final_norm if i == DEPTH - 1 else None)
    return h.reshape(batch, seq, D_MODEL)
```

```python
import functools
import math

import jax
import jax.numpy as jnp
from jax import lax
from jax.experimental import pallas as pl
from jax.experimental.pallas import tpu as pltpu

F32 = jnp.float32
BF16 = jnp.bfloat16

D_MODEL = 1024
DEPTH = 2
N_MEM = 256
D_DIFF = 512
N_DIFF_HEADS = 4
DIFF_V_DIM = 128
DIFF_QK_DIM = 64
ROPE_DIM = 16
ROPE_THETA = 500000.0
D_GMLP = 512
N_GMLP_GROUPS = 4
GMLP_GROUP_DIM = 128
CHUNK = 128
N_CROSS_HEADS = 4
CROSS_HEAD_DIM = 256
D_FF = 2816
D_IN = 3 * D_DIFF + 2 * D_GMLP
NORM_EPS = 1e-6
NEG_INF = -1e30

LANES = 128
MXU_N = 256
VMEM_LIMIT = 56 * 1024 * 1024

TM = 512
TQ = 512
ROPE_TM = 1024

NT_DIMS = (((1,), (1,)), ((), ()))


def _rms(x, g):
    return x * lax.rsqrt(jnp.mean(x * x, axis=-1, keepdims=True) + NORM_EPS) * g


def _dot(a, b):
    return jnp.dot(a, b, preferred_element_type=F32)


def _resident(shape):
    return pl.BlockSpec(shape, lambda *_: (0,) * len(shape), pipeline_mode=pl.Buffered(1))


def _params(n_axes):
    return pltpu.CompilerParams(
        dimension_semantics=("parallel",) * n_axes, vmem_limit_bytes=VMEM_LIMIT)


def _rope_kernel(pos_ref, freq_ref, ma_ref, mb_ref, cos_ref, sa_ref, sb_ref):
    ang = pos_ref[...].astype(F32) * freq_ref[...]
    cos_ref[...] = jnp.cos(ang)
    s = jnp.sin(ang)
    sa_ref[...] = s * ma_ref[...]
    sb_ref[...] = s * mb_ref[...]


def _rope_tables(positions):
    t = positions.size
    half = ROPE_DIM // 2
    inv_freq = 1.0 / (ROPE_THETA ** (jnp.arange(0, ROPE_DIM, 2, dtype=F32) / ROPE_DIM))
    r = jnp.arange(LANES) % DIFF_QK_DIM
    freq = jnp.where(r < ROPE_DIM, inv_freq[r % half], 0.0).astype(F32)[None, :]
    ma = jnp.where(r < half, -1.0, 0.0).astype(F32)[None, :]
    mb = jnp.where((r >= half) & (r < ROPE_DIM), 1.0, 0.0).astype(F32)[None, :]
    vec = pl.BlockSpec((1, LANES), lambda i: (0, 0))
    tab = pl.BlockSpec((ROPE_TM, LANES), lambda i: (i, 0))
    out = jax.ShapeDtypeStruct((t, LANES), F32)
    return pl.pallas_call(
        _rope_kernel,
        grid=(t // ROPE_TM,),
        in_specs=[pl.BlockSpec((ROPE_TM, 1), lambda i: (i, 0)), vec, vec, vec],
        out_specs=[tab, tab, tab],
        out_shape=[out, out, out],
        compiler_params=_params(1),
        name="rope_tables",
    )(positions.reshape(t, 1), freq, ma, mb)


FF_CHUNK = MXU_N


def _ffn_kernel(*refs, final):
    if final:
        x_ref, g_ref, wgu_ref, wd_ref, fn_ref, o_ref, a_ref = refs
    else:
        x_ref, g_ref, wgu_ref, wd_ref, o_ref, a_ref = refs
    x = x_ref[...]
    xn = _rms(x, g_ref[...]).astype(BF16)
    for c in range(D_FF // FF_CHUNK):
        lo = c * FF_CHUNK
        gate = _dot(xn, wgu_ref[:, lo:lo + FF_CHUNK])
        up = _dot(xn, wgu_ref[:, D_FF + lo:D_FF + lo + FF_CHUNK])
        a_ref[:, lo:lo + FF_CHUNK] = (gate * jax.nn.sigmoid(gate) * up).astype(BF16)
    y = x + 0.5 * _dot(a_ref[...], wd_ref[...])
    if final:
        y = _rms(y, fn_ref[...])
    o_ref[...] = y


def _ffn(h, norm_g, wgu, wd, final_g=None):
    t = h.shape[0]
    final = final_g is not None
    row = pl.BlockSpec((TM, D_MODEL), lambda i: (i, 0))
    in_specs = [row, _resident((1, D_MODEL)), _resident((D_MODEL, 2 * D_FF)),
                _resident((D_FF, D_MODEL))]
    args = [h, norm_g[None, :], wgu, wd]
    if final:
        in_specs.append(_resident((1, D_MODEL)))
        args.append(final_g[None, :])
    return pl.pallas_call(
        functools.partial(_ffn_kernel, final=final),
        grid=(t // TM,),
        in_specs=in_specs,
        out_specs=row,
        out_shape=jax.ShapeDtypeStruct((t, D_MODEL), F32),
        scratch_shapes=[pltpu.VMEM((TM, D_FF), BF16)],
        compiler_params=_params(1),
        name="ffn",
    )(*args)


Q_SCALE = DIFF_QK_DIM ** -0.5
SQRT_HALF = math.sqrt(0.5)


def _gelu(x):
    return 0.5 * x * (1.0 + lax.erf(x * SQRT_HALF))


def _mix_kernel(h_ref, g_ref, win_ref, cos_ref, sa_ref, sb_ref, lng_ref, lnb_ref,
                ws_ref, bias_ref, qkv_ref, gm_ref):
    hn = _rms(h_ref[...], g_ref[...]).astype(BF16)
    cos = cos_ref[...]
    sa = sa_ref[...]
    sb = sb_ref[...]

    for part, scale in ((0, Q_SCALE), (1, 1.0)):
        proj = _dot(hn, win_ref[:, part * D_DIFF:(part + 1) * D_DIFF])
        for hd in range(N_DIFF_HEADS):
            t = proj[:, hd * LANES:(hd + 1) * LANES]
            r = (t * cos + pltpu.roll(t, LANES - ROPE_DIM // 2, 1) * sa
                 + pltpu.roll(t, ROPE_DIM // 2, 1) * sb)
            lo = part * D_DIFF + hd * LANES
            qkv_ref[:, lo:lo + LANES] = (r * scale).astype(BF16)
    qkv_ref[:, 2 * D_DIFF:] = _dot(hn, win_ref[:, 2 * D_DIFF:3 * D_DIFF]).astype(BF16)

    gu = _gelu(_dot(hn, win_ref[:, 3 * D_DIFF:3 * D_DIFF + D_GMLP]))
    gv = _gelu(_dot(hn, win_ref[:, 3 * D_DIFF + D_GMLP:]))
    tril = (lax.broadcasted_iota(jnp.int32, (CHUNK, CHUNK), 1)
            <= lax.broadcasted_iota(jnp.int32, (CHUNK, CHUNK), 0))
    for g in range(N_GMLP_GROUPS):
        cols = slice(g * GMLP_GROUP_DIM, (g + 1) * GMLP_GROUP_DIM)
        v = gv[:, cols]
        xc = v - jnp.mean(v, axis=-1, keepdims=True)
        vn = xc * lax.rsqrt(jnp.mean(xc * xc, axis=-1, keepdims=True) + NORM_EPS)
        vn = (vn * lng_ref[:, cols] + lnb_ref[:, cols]).astype(BF16)
        w = jnp.where(tril, ws_ref[g], 0.0).astype(BF16)
        for c in range(TM // CHUNK):
            rows = slice(c * CHUNK, (c + 1) * CHUNK)
            mixed = _dot(w, vn[rows, :]) + bias_ref[g]
            gm_ref[rows, cols] = (gu[rows, cols] * mixed).astype(BF16)


def _mix(h, norm_g, w_in, cos, sa, sb, ln_g, ln_b, ws, bs):
    t = h.shape[0]
    bias = jnp.broadcast_to(bs[:, :, None], (N_GMLP_GROUPS, CHUNK, GMLP_GROUP_DIM))
    tab = pl.BlockSpec((TM, LANES), lambda i: (i, 0))
    return pl.pallas_call(
        _mix_kernel,
        grid=(t // TM,),
        in_specs=[pl.BlockSpec((TM, D_MODEL), lambda i: (i, 0)),
                  _resident((1, D_MODEL)), _resident((D_MODEL, D_IN)),
                  tab, tab, tab,
                  _resident((1, D_GMLP)), _resident((1, D_GMLP)),
                  _resident((N_GMLP_GROUPS, CHUNK, CHUNK)),
                  _resident((N_GMLP_GROUPS, CHUNK, GMLP_GROUP_DIM))],
        out_specs=[pl.BlockSpec((TM, 3 * D_DIFF), lambda i: (i, 0)),
                   pl.BlockSpec((TM, D_GMLP), lambda i: (i, 0))],
        out_shape=[jax.ShapeDtypeStruct((t, 3 * D_DIFF), BF16),
                   jax.ShapeDtypeStruct((t, D_GMLP), BF16)],
        compiler_params=_params(1),
        name="mix_proj",
    )(h, norm_g[None, :], w_in, cos, sa, sb, ln_g.reshape(1, D_GMLP),
      ln_b.reshape(1, D_GMLP), ws, bias)


V_EXT = 2 * DIFF_V_DIM


def _attn_kernel(q_ref, k_ref, v_ref, lam_ref, sub_ref, o_ref,
                 vext_ref, m1_ref, a1_ref, m2_ref, a2_ref, *, lam_init):
    qi = pl.program_id(2)

    @pl.when(qi == 0)
    def _():
        vext_ref[:, :DIFF_V_DIM] = v_ref[0]
        vext_ref[:, DIFF_V_DIM:] = jnp.ones((v_ref.shape[1], DIFF_V_DIM), BF16)

    q = q_ref[0]
    lane = lax.broadcasted_iota(jnp.int32, q.shape, 1)
    zero = jnp.zeros_like(q)
    q_maps = (jnp.where(lane < DIFF_QK_DIM, q, zero), jnp.where(lane >= DIFF_QK_DIM, q, zero))
    stats = ((m1_ref, a1_ref), (m2_ref, a2_ref))

    def block(off, first):
        kb = k_ref[0, pl.ds(off, TQ), :]
        vb = vext_ref[pl.ds(off, TQ), :]
        for qm, (m_ref, a_ref) in zip(q_maps, stats):
            s = lax.dot_general(qm, kb, NT_DIMS, preferred_element_type=F32)
            if first:
                causal = (lax.broadcasted_iota(jnp.int32, (TQ, TQ), 1)
                          <= lax.broadcasted_iota(jnp.int32, (TQ, TQ), 0))
                s = jnp.where(causal, s, NEG_INF)
            cols = [s[:, c * LANES:(c + 1) * LANES] for c in range(TQ // LANES)]
            m_cur = functools.reduce(jnp.maximum, cols)
            m_cur = jnp.max(m_cur, axis=-1, keepdims=True)
            if first:
                m_new = jnp.broadcast_to(m_cur, (TQ, LANES))
            else:
                m_old = m_ref[...]
                m_new = jnp.maximum(m_old, m_cur)
                alpha = jnp.exp(m_old - m_new)
            p = jnp.concatenate([jnp.exp(c - m_new) for c in cols], axis=1).astype(BF16)
            pv = _dot(p, vb)
            if first:
                a_ref[...] = pv
            else:
                a_ref[...] = jnp.concatenate([alpha, alpha], axis=1) * a_ref[...] + pv
            m_ref[...] = m_new

    block(pl.multiple_of(qi * TQ, TQ), True)

    def body(j, carry):
        block(pl.multiple_of(j * TQ, TQ), False)
        return carry

    lax.fori_loop(0, qi, body, 0)

    lv = lam_ref[...]
    lam = (jnp.exp(jnp.sum(lv[0:1] * lv[1:2], axis=-1, keepdims=True))
           - jnp.exp(jnp.sum(lv[2:3] * lv[3:4], axis=-1, keepdims=True)) + lam_init)
    o = (a1_ref[:, :DIFF_V_DIM] / a1_ref[:, DIFF_V_DIM:]
         - lam * (a2_ref[:, :DIFF_V_DIM] / a2_ref[:, DIFF_V_DIM:]))
    o_ref[0] = (_rms(o, sub_ref[...]) * (1.0 - lam_init)).astype(BF16)


def _diff_attention(qkv, lam_vecs, subln, lam_init, batch, seq):
    h = N_DIFF_HEADS
    kv_spec = lambda part: pl.BlockSpec(
        (1, seq, LANES), lambda b, hd, qi: (b, 0, part * h + hd))
    return pl.pallas_call(
        functools.partial(_attn_kernel, lam_init=lam_init),
        grid=(batch, h, seq // TQ),
        in_specs=[pl.BlockSpec((1, TQ, LANES), lambda b, hd, qi: (b, qi, hd)),
                  kv_spec(1), kv_spec(2),
                  pl.BlockSpec((4, DIFF_QK_DIM), lambda b, hd, qi: (0, 0)),
                  pl.BlockSpec((1, DIFF_V_DIM), lambda b, hd, qi: (0, 0))],
        out_specs=pl.BlockSpec((1, TQ, LANES), lambda b, hd, qi: (b, qi, hd)),
        out_shape=jax.ShapeDtypeStruct((batch, seq, D_DIFF), BF16),
        scratch_shapes=[pltpu.VMEM((seq, V_EXT), BF16)]
        + [pltpu.VMEM((TQ, LANES), F32), pltpu.VMEM((TQ, V_EXT), F32)] * 2,
        compiler_params=pltpu.CompilerParams(
            dimension_semantics=("parallel", "parallel", "arbitrary"),
            vmem_limit_bytes=VMEM_LIMIT),
        name="diff_attn",
    )(qkv, qkv, qkv, lam_vecs, subln[None, :])


def _memkv_kernel(mem_ref, g_ref, wkv_ref, k_ref, v_ref):
    mn = _rms(mem_ref[...], g_ref[...]).astype(BF16)
    kv = _dot(mn, wkv_ref[...])
    k_ref[...] = kv[:, :D_MODEL].astype(BF16)
    v_ref[...] = kv[:, D_MODEL:].astype(BF16)


def _mem_kv(mem2d, norm_g, wkv):
    t = mem2d.shape[0]
    row = pl.BlockSpec((N_MEM, D_MODEL), lambda i: (i, 0))
    out = jax.ShapeDtypeStruct((t, D_MODEL), BF16)
    return pl.pallas_call(
        _memkv_kernel,
        grid=(t // N_MEM,),
        in_specs=[row, _resident((1, D_MODEL)), _resident((D_MODEL, 2 * D_MODEL))],
        out_specs=[row, row],
        out_shape=[out, out],
        compiler_params=_params(1),
        name="mem_kv",
    )(mem2d, norm_g[None, :], wkv)


CROSS_SCALE = CROSS_HEAD_DIM ** -0.5


def _outcross_kernel(h_ref, attn_ref, gm_ref, woa_ref, wog_ref, cn_ref, wq_ref,
                     k_ref, v_ref, wo_ref, o_ref, ctx_ref):
    h2 = h_ref[...] + _dot(attn_ref[...], woa_ref[...]) + _dot(gm_ref[...], wog_ref[...])
    hn = _rms(h2, cn_ref[...]).astype(BF16)
    q = (_dot(hn, wq_ref[...]) * CROSS_SCALE).astype(BF16)
    for hd in range(N_CROSS_HEADS):
        cols = slice(hd * CROSS_HEAD_DIM, (hd + 1) * CROSS_HEAD_DIM)
        sc = lax.dot_general(q[:, cols], k_ref[:, cols], NT_DIMS, preferred_element_type=F32)
        p = jnp.exp(sc - jnp.max(sc, axis=-1, keepdims=True))
        p = p / jnp.sum(p, axis=-1, keepdims=True)
        ctx_ref[:, cols] = _dot(p.astype(BF16), v_ref[:, cols]).astype(BF16)
    o_ref[...] = h2 + _dot(ctx_ref[...], wo_ref[...])


def _out_cross(h, attn, gm, w_out, norm_g, wq, kmem, vmem, wo, seq):
    t = h.shape[0]
    row = pl.BlockSpec((TM, D_MODEL), lambda i: (i, 0))
    half = pl.BlockSpec((TM, D_DIFF), lambda i: (i, 0))
    mem = pl.BlockSpec((N_MEM, D_MODEL), lambda i: (i * TM // seq, 0))
    return pl.pallas_call(
        _outcross_kernel,
        grid=(t // TM,),
        in_specs=[row, half, half,
                  _resident((D_DIFF, D_MODEL)), _resident((D_GMLP, D_MODEL)),
                  _resident((1, D_MODEL)), _resident((D_MODEL, D_MODEL)),
                  mem, mem, _resident((D_MODEL, D_MODEL))],
        out_specs=row,
        out_shape=jax.ShapeDtypeStruct((t, D_MODEL), F32),
        scratch_shapes=[pltpu.VMEM((TM, D_MODEL), BF16)],
        compiler_params=_params(1),
        name="out_cross",
    )(h, attn, gm, w_out[:D_DIFF], w_out[D_DIFF:], norm_g[None, :], wq, kmem, vmem, wo)


def kernel(x, mem, positions, ffn1_norm, ffn1_wgu, ffn1_wd, mix_norm, w_in, lam_q1, lam_k1,
           lam_q2, lam_k2, subln, gmlp_ln_g, gmlp_ln_b, gmlp_ws, gmlp_bs, w_out, cross_norm,
           mem_norm, cross_wq, cross_wkv, cross_wo, ffn2_norm, ffn2_wgu, ffn2_wd, final_norm):
    batch, seq, _ = x.shape
    t = batch * seq
    assert seq % TQ == 0 and seq % TM == 0 and t % ROPE_TM == 0
    cos, sa, sb = _rope_tables(positions)
    h = x.reshape(t, D_MODEL)
    mem2d = mem.reshape(batch * N_MEM, D_MODEL)
    bf = lambda w: w.astype(BF16)
    for i in range(DEPTH):
        lam_init = 0.8 - 0.6 * math.exp(-0.3 * i)
        h = _ffn(h, ffn1_norm[i], bf(ffn1_wgu[i]), bf(ffn1_wd[i]))
        qkv, gm = _mix(h, mix_norm[i], bf(w_in[i]), cos, sa, sb, gmlp_ln_g[i], gmlp_ln_b[i],
                       gmlp_ws[i], gmlp_bs[i])
        lam_vecs = jnp.stack([lam_q1[i], lam_k1[i], lam_q2[i], lam_k2[i]])
        attn = _diff_attention(qkv.reshape(batch, seq, 3 * D_DIFF), lam_vecs, subln[i],
                               lam_init, batch, seq)
        kmem, vmem = _mem_kv(mem2d, mem_norm[i], bf(cross_wkv[i]))
        h = _out_cross(h, attn.reshape(t, D_DIFF), gm, bf(w_out[i]), cross_norm[i],
                       bf(cross_wq[i]), kmem, vmem, bf(cross_wo[i]), seq)
        h = _ffn(h, ffn2_norm[i], bf(ffn2_wgu[i]), bf(ffn2_wd[i]),
                 final_g=final_norm if i == DEPTH - 1 else None)
    return h.reshape(batch, seq, D_MODEL)
```

```python
import functools
import math

import jax
import jax.numpy as jnp
from jax import lax
from jax.experimental import pallas as pl
from jax.experimental.pallas import tpu as pltpu

F32 = jnp.float32
BF16 = jnp.bfloat16

D_MODEL = 1024
DEPTH = 2
N_MEM = 256
D_DIFF = 512
N_DIFF_HEADS = 4
DIFF_V_DIM = 128
DIFF_QK_DIM = 64
ROPE_DIM = 16
ROPE_THETA = 500000.0
D_GMLP = 512
N_GMLP_GROUPS = 4
GMLP_GROUP_DIM = 128
CHUNK = 128
N_CROSS_HEADS = 4
CROSS_HEAD_DIM = 256
D_FF = 2816
D_IN = 3 * D_DIFF + 2 * D_GMLP
NORM_EPS = 1e-6
NEG_INF = -1e30

LANES = 128
MXU_N = 256
VMEM_LIMIT = 56 * 1024 * 1024

TM = 512
TQ = 512
ROPE_TM = 1024

NT_DIMS = (((1,), (1,)), ((), ()))


def _rms(x, g):
    return x * lax.rsqrt(jnp.mean(x * x, axis=-1, keepdims=True) + NORM_EPS) * g


def _dot(a, b):
    return jnp.dot(a, b, preferred_element_type=F32)


def _resident(shape):
    return pl.BlockSpec(shape, lambda *_: (0,) * len(shape), pipeline_mode=pl.Buffered(1))


def _params(n_axes):
    return pltpu.CompilerParams(
        dimension_semantics=("parallel",) * n_axes, vmem_limit_bytes=VMEM_LIMIT)


def _rope_kernel(pos_ref, freq_ref, ma_ref, mb_ref, cos_ref, sa_ref, sb_ref):
    ang = pos_ref[...].astype(F32) * freq_ref[...]
    cos_ref[...] = jnp.cos(ang)
    s = jnp.sin(ang)
    sa_ref[...] = s * ma_ref[...]
    sb_ref[...] = s * mb_ref[...]


def _rope_tables(positions):
    t = positions.size
    half = ROPE_DIM // 2
    inv_freq = 1.0 / (ROPE_THETA ** (jnp.arange(0, ROPE_DIM, 2, dtype=F32) / ROPE_DIM))
    r = jnp.arange(LANES) % DIFF_QK_DIM
    freq = jnp.where(r < ROPE_DIM, inv_freq[r % half], 0.0).astype(F32)[None, :]
    ma = jnp.where(r < half, -1.0, 0.0).astype(F32)[None, :]
    mb = jnp.where((r >= half) & (r < ROPE_DIM), 1.0, 0.0).astype(F32)[None, :]
    vec = pl.BlockSpec((1, LANES), lambda i: (0, 0))
    tab = pl.BlockSpec((ROPE_TM, LANES), lambda i: (i, 0))
    out = jax.ShapeDtypeStruct((t, LANES), F32)
    return pl.pallas_call(
        _rope_kernel,
        grid=(t // ROPE_TM,),
        in_specs=[pl.BlockSpec((ROPE_TM, 1), lambda i: (i, 0)), vec, vec, vec],
        out_specs=[tab, tab, tab],
        out_shape=[out, out, out],
        compiler_params=_params(1),
        name="rope_tables",
    )(positions.reshape(t, 1), freq, ma, mb)


FF_CHUNK = MXU_N


def _ffn_kernel(*refs, final):
    if final:
        x_ref, g_ref, wgu_ref, wd_ref, fn_ref, o_ref, a_ref = refs
    else:
        x_ref, g_ref, wgu_ref, wd_ref, o_ref, a_ref = refs
    x = x_ref[...]
    xn = _rms(x, g_ref[...]).astype(BF16)
    for c in range(D_FF // FF_CHUNK):
        lo = c * FF_CHUNK
        gate = _dot(xn, wgu_ref[:, lo:lo + FF_CHUNK])
        up = _dot(xn, wgu_ref[:, D_FF + lo:D_FF + lo + FF_CHUNK])
        a_ref[:, lo:lo + FF_CHUNK] = (gate * jax.nn.sigmoid(gate) * up).astype(BF16)
    y = x + 0.5 * _dot(a_ref[...], wd_ref[...])
    if final:
        y = _rms(y, fn_ref[...])
    o_ref[...] = y


def _ffn(h, norm_g, wgu, wd, final_g=None):
    t = h.shape[0]
    final = final_g is not None
    row = pl.BlockSpec((TM, D_MODEL), lambda i: (i, 0))
    in_specs = [row, _resident((1, D_MODEL)), _resident((D_MODEL, 2 * D_FF)),
                _resident((D_FF, D_MODEL))]
    args = [h, norm_g[None, :], wgu, wd]
    if final:
        in_specs.append(_resident((1, D_MODEL)))
        args.append(final_g[None, :])
    return pl.pallas_call(
        functools.partial(_ffn_kernel, final=final),
        grid=(t // TM,),
        in_specs=in_specs,
        out_specs=row,
        out_shape=jax.ShapeDtypeStruct((t, D_MODEL), F32),
        scratch_shapes=[pltpu.VMEM((TM, D_FF), BF16)],
        compiler_params=_params(1),
        name="ffn",
    )(*args)


Q_SCALE = DIFF_QK_DIM ** -0.5
SQRT_HALF = math.sqrt(0.5)


def _gelu(x):
    return 0.5 * x * (1.0 + lax.erf(x * SQRT_HALF))


def _mix_kernel(h_ref, g_ref, win_ref, cos_ref, sa_ref, sb_ref, lng_ref, lnb_ref,
                ws_ref, bias_ref, qkv_ref, gm_ref):
    hn = _rms(h_ref[...], g_ref[...]).astype(BF16)
    cos = cos_ref[...]
    sa = sa_ref[...]
    sb = sb_ref[...]

    for part, scale in ((0, Q_SCALE), (1, 1.0)):
        proj = _dot(hn, win_ref[:, part * D_DIFF:(part + 1) * D_DIFF])
        for hd in range(N_DIFF_HEADS):
            t = proj[:, hd * LANES:(hd + 1) * LANES]
            r = (t * cos + pltpu.roll(t, LANES - ROPE_DIM // 2, 1) * sa
                 + pltpu.roll(t, ROPE_DIM // 2, 1) * sb)
            lo = part * D_DIFF + hd * LANES
            qkv_ref[:, lo:lo + LANES] = (r * scale).astype(BF16)
    qkv_ref[:, 2 * D_DIFF:] = _dot(hn, win_ref[:, 2 * D_DIFF:3 * D_DIFF]).astype(BF16)

    gu = _gelu(_dot(hn, win_ref[:, 3 * D_DIFF:3 * D_DIFF + D_GMLP]))
    gv = _gelu(_dot(hn, win_ref[:, 3 * D_DIFF + D_GMLP:]))
    tril = (lax.broadcasted_iota(jnp.int32, (CHUNK, CHUNK), 1)
            <= lax.broadcasted_iota(jnp.int32, (CHUNK, CHUNK), 0))
    for g in range(N_GMLP_GROUPS):
        cols = slice(g * GMLP_GROUP_DIM, (g + 1) * GMLP_GROUP_DIM)
        v = gv[:, cols]
        xc = v - jnp.mean(v, axis=-1, keepdims=True)
        vn = xc * lax.rsqrt(jnp.mean(xc * xc, axis=-1, keepdims=True) + NORM_EPS)
        vn = (vn * lng_ref[:, cols] + lnb_ref[:, cols]).astype(BF16)
        w = jnp.where(tril, ws_ref[g], 0.0).astype(BF16)
        for c in range(TM // CHUNK):
            rows = slice(c * CHUNK, (c + 1) * CHUNK)
            mixed = _dot(w, vn[rows, :]) + bias_ref[g]
            gm_ref[rows, cols] = (gu[rows, cols] * mixed).astype(BF16)


def _mix(h, norm_g, w_in, cos, sa, sb, ln_g, ln_b, ws, bs):
    t = h.shape[0]
    bias = jnp.broadcast_to(bs[:, :, None], (N_GMLP_GROUPS, CHUNK, GMLP_GROUP_DIM))
    tab = pl.BlockSpec((TM, LANES), lambda i: (i, 0))
    return pl.pallas_call(
        _mix_kernel,
        grid=(t // TM,),
        in_specs=[pl.BlockSpec((TM, D_MODEL), lambda i: (i, 0)),
                  _resident((1, D_MODEL)), _resident((D_MODEL, D_IN)),
                  tab, tab, tab,
                  _resident((1, D_GMLP)), _resident((1, D_GMLP)),
                  _resident((N_GMLP_GROUPS, CHUNK, CHUNK)),
                  _resident((N_GMLP_GROUPS, CHUNK, GMLP_GROUP_DIM))],
        out_specs=[pl.BlockSpec((TM, 3 * D_DIFF), lambda i: (i, 0)),
                   pl.BlockSpec((TM, D_GMLP), lambda i: (i, 0))],
        out_shape=[jax.ShapeDtypeStruct((t, 3 * D_DIFF), BF16),
                   jax.ShapeDtypeStruct((t, D_GMLP), BF16)],
        compiler_params=_params(1),
        name="mix_proj",
    )(h, norm_g[None, :], w_in, cos, sa, sb, ln_g.reshape(1, D_GMLP),
      ln_b.reshape(1, D_GMLP), ws, bias)


V_EXT = 2 * DIFF_V_DIM


def _attn_kernel(q_ref, k_ref, v_ref, lam_ref, sub_ref, o_ref,
                 vext_ref, qq_ref, sa_ref, sb_ref, m_ref, a_ref, *, lam_init):
    qi = pl.program_id(2)

    @pl.when(qi == 0)
    def _():
        vext_ref[:, :DIFF_V_DIM] = v_ref[0]
        vext_ref[:, DIFF_V_DIM:] = jnp.ones((v_ref.shape[1], DIFF_V_DIM), BF16)

    q = q_ref[0]
    lane = lax.broadcasted_iota(jnp.int32, q.shape, 1)
    zero = jnp.zeros_like(q)
    qq_ref[:TQ, :] = jnp.where(lane < DIFF_QK_DIM, q, zero)
    qq_ref[TQ:, :] = jnp.where(lane >= DIFF_QK_DIM, q, zero)

    halves = (slice(0, TQ), slice(TQ, 2 * TQ))

    def block_off(j):
        return pl.multiple_of(j * TQ, TQ)

    def scores(off, s_ref):
        kb = k_ref[0, pl.ds(off, TQ), :]
        for rows in halves:
            s_ref[rows, :] = lax.dot_general(qq_ref[rows, :], kb, NT_DIMS,
                                             preferred_element_type=F32)

    def update(off, s_ref, diagonal):
        vb = vext_ref[pl.ds(off, TQ), :]
        probs, alphas = [], []
        for rows in halves:
            s = s_ref[rows, :]
            if diagonal:
                causal = (lax.broadcasted_iota(jnp.int32, (TQ, TQ), 1)
                          <= lax.broadcasted_iota(jnp.int32, (TQ, TQ), 0))
                s = jnp.where(causal, s, NEG_INF)
            cols = [s[:, c * LANES:(c + 1) * LANES] for c in range(TQ // LANES)]
            m_cur = jnp.max(functools.reduce(jnp.maximum, cols), axis=-1, keepdims=True)
            m_old = m_ref[rows, :]
            m_new = jnp.maximum(m_old, m_cur)
            alphas.append(jnp.exp(m_old - m_new))
            m_ref[rows, :] = m_new
            probs.append(
                jnp.concatenate([jnp.exp(c - m_new) for c in cols], axis=1).astype(BF16))
        for rows, p, alpha in zip(halves, probs, alphas):
            pv = _dot(p, vb)
            a_ref[rows, :] = jnp.concatenate([alpha, alpha], axis=1) * a_ref[rows, :] + pv

    m_ref[...] = jnp.full(m_ref.shape, NEG_INF, F32)
    a_ref[...] = jnp.zeros(a_ref.shape, F32)
    scores(block_off(0), sa_ref)

    def pair(t, carry):
        j = 2 * t
        scores(block_off(j + 1), sb_ref)
        update(block_off(j), sa_ref, False)
        scores(block_off(j + 2), sa_ref)
        update(block_off(j + 1), sb_ref, False)
        return carry

    lax.fori_loop(0, qi // 2, pair, 0)

    @pl.when(qi % 2 == 1)
    def _():
        scores(block_off(qi), sb_ref)
        update(block_off(qi - 1), sa_ref, False)
        update(block_off(qi), sb_ref, True)

    @pl.when(qi % 2 == 0)
    def _():
        update(block_off(qi), sa_ref, True)

    lv = lam_ref[...]
    lam = (jnp.exp(jnp.sum(lv[0:1] * lv[1:2], axis=-1, keepdims=True))
           - jnp.exp(jnp.sum(lv[2:3] * lv[3:4], axis=-1, keepdims=True)) + lam_init)
    o = (a_ref[:TQ, :DIFF_V_DIM] / a_ref[:TQ, DIFF_V_DIM:]
         - lam * (a_ref[TQ:, :DIFF_V_DIM] / a_ref[TQ:, DIFF_V_DIM:]))
    o_ref[0] = (_rms(o, sub_ref[...]) * (1.0 - lam_init)).astype(BF16)


def _diff_attention(qkv, lam_vecs, subln, lam_init, batch, seq):
    h = N_DIFF_HEADS
    kv_spec = lambda part: pl.BlockSpec(
        (1, seq, LANES), lambda b, hd, qi: (b, 0, part * h + hd))
    return pl.pallas_call(
        functools.partial(_attn_kernel, lam_init=lam_init),
        grid=(batch, h, seq // TQ),
        in_specs=[pl.BlockSpec((1, TQ, LANES), lambda b, hd, qi: (b, qi, hd)),
                  kv_spec(1), kv_spec(2),
                  pl.BlockSpec((4, DIFF_QK_DIM), lambda b, hd, qi: (0, 0)),
                  pl.BlockSpec((1, DIFF_V_DIM), lambda b, hd, qi: (0, 0))],
        out_specs=pl.BlockSpec((1, TQ, LANES), lambda b, hd, qi: (b, qi, hd)),
        out_shape=jax.ShapeDtypeStruct((batch, seq, D_DIFF), BF16),
        scratch_shapes=[pltpu.VMEM((seq, V_EXT), BF16), pltpu.VMEM((2 * TQ, LANES), BF16),
                        pltpu.VMEM((2 * TQ, TQ), F32), pltpu.VMEM((2 * TQ, TQ), F32),
                        pltpu.VMEM((2 * TQ, LANES), F32), pltpu.VMEM((2 * TQ, V_EXT), F32)],
        compiler_params=pltpu.CompilerParams(
            dimension_semantics=("parallel", "parallel", "arbitrary"),
            vmem_limit_bytes=VMEM_LIMIT),
        name="diff_attn",
    )(qkv, qkv, qkv, lam_vecs, subln[None, :])


def _memkv_kernel(mem_ref, g_ref, wkv_ref, k_ref, v_ref):
    mn = _rms(mem_ref[...], g_ref[...]).astype(BF16)
    kv = _dot(mn, wkv_ref[...])
    k_ref[...] = kv[:, :D_MODEL].astype(BF16)
    v_ref[...] = kv[:, D_MODEL:].astype(BF16)


def _mem_kv(mem2d, norm_g, wkv):
    t = mem2d.shape[0]
    row = pl.BlockSpec((N_MEM, D_MODEL), lambda i: (i, 0))
    out = jax.ShapeDtypeStruct((t, D_MODEL), BF16)
    return pl.pallas_call(
        _memkv_kernel,
        grid=(t // N_MEM,),
        in_specs=[row, _resident((1, D_MODEL)), _resident((D_MODEL, 2 * D_MODEL))],
        out_specs=[row, row],
        out_shape=[out, out],
        compiler_params=_params(1),
        name="mem_kv",
    )(mem2d, norm_g[None, :], wkv)


CROSS_SCALE = CROSS_HEAD_DIM ** -0.5


def _outcross_kernel(h_ref, attn_ref, gm_ref, woa_ref, wog_ref, cn_ref, wq_ref,
                     k_ref, v_ref, wo_ref, o_ref, ctx_ref):
    h2 = h_ref[...] + _dot(attn_ref[...], woa_ref[...]) + _dot(gm_ref[...], wog_ref[...])
    hn = _rms(h2, cn_ref[...]).astype(BF16)
    q = (_dot(hn, wq_ref[...]) * CROSS_SCALE).astype(BF16)
    for hd in range(N_CROSS_HEADS):
        cols = slice(hd * CROSS_HEAD_DIM, (hd + 1) * CROSS_HEAD_DIM)
        sc = lax.dot_general(q[:, cols], k_ref[:, cols], NT_DIMS, preferred_element_type=F32)
        p = jnp.exp(sc - jnp.max(sc, axis=-1, keepdims=True))
        p = p / jnp.sum(p, axis=-1, keepdims=True)
        ctx_ref[:, cols] = _dot(p.astype(BF16), v_ref[:, cols]).astype(BF16)
    o_ref[...] = h2 + _dot(ctx_ref[...], wo_ref[...])


def _out_cross(h, attn, gm, w_out, norm_g, wq, kmem, vmem, wo, seq):
    t = h.shape[0]
    row = pl.BlockSpec((TM, D_MODEL), lambda i: (i, 0))
    half = pl.BlockSpec((TM, D_DIFF), lambda i: (i, 0))
    mem = pl.BlockSpec((N_MEM, D_MODEL), lambda i: (i * TM // seq, 0))
    return pl.pallas_call(
        _outcross_kernel,
        grid=(t // TM,),
        in_specs=[row, half, half,
                  _resident((D_DIFF, D_MODEL)), _resident((D_GMLP, D_MODEL)),
                  _resident((1, D_MODEL)), _resident((D_MODEL, D_MODEL)),
                  mem, mem, _resident((D_MODEL, D_MODEL))],
        out_specs=row,
        out_shape=jax.ShapeDtypeStruct((t, D_MODEL), F32),
        scratch_shapes=[pltpu.VMEM((TM, D_MODEL), BF16)],
        compiler_params=_params(1),
        name="out_cross",
    )(h, attn, gm, w_out[:D_DIFF], w_out[D_DIFF:], norm_g[None, :], wq, kmem, vmem, wo)


def kernel(x, mem, positions, ffn1_norm, ffn1_wgu, ffn1_wd, mix_norm, w_in, lam_q1, lam_k1,
           lam_q2, lam_k2, subln, gmlp_ln_g, gmlp_ln_b, gmlp_ws, gmlp_bs, w_out, cross_norm,
           mem_norm, cross_wq, cross_wkv, cross_wo, ffn2_norm, ffn2_wgu, ffn2_wd, final_norm):
    batch, seq, _ = x.shape
    t = batch * seq
    assert seq % TQ == 0 and seq % TM == 0 and t % ROPE_TM == 0
    cos, sa, sb = _rope_tables(positions)
    h = x.reshape(t, D_MODEL)
    mem2d = mem.reshape(batch * N_MEM, D_MODEL)
    bf = lambda w: w.astype(BF16)
    for i in range(DEPTH):
        lam_init = 0.8 - 0.6 * math.exp(-0.3 * i)
        h = _ffn(h, ffn1_norm[i], bf(ffn1_wgu[i]), bf(ffn1_wd[i]))
        qkv, gm = _mix(h, mix_norm[i], bf(w_in[i]), cos, sa, sb, gmlp_ln_g[i], gmlp_ln_b[i],
                       gmlp_ws[i], gmlp_bs[i])
        lam_vecs = jnp.stack([lam_q1[i], lam_k1[i], lam_q2[i], lam_k2[i]])
        attn = _diff_attention(qkv.reshape(batch, seq, 3 * D_DIFF), lam_vecs, subln[i],
                               lam_init, batch, seq)
        kmem, vmem = _mem_kv(mem2d, mem_norm[i], bf(cross_wkv[i]))
        h = _out_cross(h, attn.reshape(t, D_DIFF), gm, bf(w_out[i]), cross_norm[i],
                       bf(cross_wq[i]), kmem, vmem, bf(cross_wo[i]), seq)
        h = _ffn(h, ffn2_norm[i], bf(ffn2_wgu[i]), bf(ffn2_wd[i]),
                 final_g=final_norm if i == DEPTH - 1 else None)
    return h.reshape(batch, seq, D_MODEL)
```

```python
import functools
import math

import jax
import jax.numpy as jnp
from jax import lax
from jax.experimental import pallas as pl
from jax.experimental.pallas import tpu as pltpu

F32 = jnp.float32
BF16 = jnp.bfloat16

D_MODEL = 1024
DEPTH = 2
N_MEM = 256
D_DIFF = 512
N_DIFF_HEADS = 4
DIFF_V_DIM = 128
DIFF_QK_DIM = 64
ROPE_DIM = 16
ROPE_THETA = 500000.0
D_GMLP = 512
N_GMLP_GROUPS = 4
GMLP_GROUP_DIM = 128
CHUNK = 128
N_CROSS_HEADS = 4
CROSS_HEAD_DIM = 256
D_FF = 2816
D_IN = 3 * D_DIFF + 2 * D_GMLP
NORM_EPS = 1e-6
NEG_INF = -1e30

LANES = 128
MXU_N = 256
VMEM_LIMIT = 56 * 1024 * 1024

TM = 512
TQ = 512
ROPE_TM = 1024

NT_DIMS = (((1,), (1,)), ((), ()))


def _rms(x, g):
    return x * lax.rsqrt(jnp.mean(x * x, axis=-1, keepdims=True) + NORM_EPS) * g


def _dot(a, b):
    return jnp.dot(a, b, preferred_element_type=F32)


def _resident(shape, *lead):
    index = tuple(lead) + (0,) * len(shape)
    return pl.BlockSpec((None,) * len(lead) + tuple(shape), lambda *_: index,
                        pipeline_mode=pl.Buffered(1))


def _params(*semantics):
    return pltpu.CompilerParams(dimension_semantics=semantics, vmem_limit_bytes=VMEM_LIMIT)


def _rope_kernel(pos_ref, freq_ref, ma_ref, mb_ref, cos_ref, sa_ref, sb_ref):
    ang = pos_ref[...].astype(F32) * freq_ref[...]
    cos_ref[...] = jnp.cos(ang)
    s = jnp.sin(ang)
    sa_ref[...] = s * ma_ref[...]
    sb_ref[...] = s * mb_ref[...]


def _rope_tables(positions):
    t = positions.size
    half = ROPE_DIM // 2
    inv_freq = 1.0 / (ROPE_THETA ** (jnp.arange(0, ROPE_DIM, 2, dtype=F32) / ROPE_DIM))
    r = jnp.arange(LANES) % DIFF_QK_DIM
    freq = jnp.where(r < ROPE_DIM, inv_freq[r % half], 0.0).astype(F32)[None, :]
    ma = jnp.where(r < half, -1.0, 0.0).astype(F32)[None, :]
    mb = jnp.where((r >= half) & (r < ROPE_DIM), 1.0, 0.0).astype(F32)[None, :]
    vec = pl.BlockSpec((1, LANES), lambda i: (0, 0))
    tab = pl.BlockSpec((ROPE_TM, LANES), lambda i: (i, 0))
    out = jax.ShapeDtypeStruct((t, LANES), F32)
    return pl.pallas_call(
        _rope_kernel,
        grid=(t // ROPE_TM,),
        in_specs=[pl.BlockSpec((ROPE_TM, 1), lambda i: (i, 0)), vec, vec, vec],
        out_specs=[tab, tab, tab],
        out_shape=[out, out, out],
        compiler_params=_params("parallel"),
        name="rope_tables",
    )(positions.reshape(t, 1), freq, ma, mb)


FF_CHUNK = MXU_N


def _ffn_kernel(*refs, final):
    if final:
        x_ref, g_ref, wgu_ref, wd_ref, fn_ref, o_ref, a_ref = refs
    else:
        x_ref, g_ref, wgu_ref, wd_ref, o_ref, a_ref = refs
    x = x_ref[...]
    xn = _rms(x, g_ref[...]).astype(BF16)
    for c in range(D_FF // FF_CHUNK):
        lo = c * FF_CHUNK
        gate = _dot(xn, wgu_ref[:, lo:lo + FF_CHUNK])
        up = _dot(xn, wgu_ref[:, D_FF + lo:D_FF + lo + FF_CHUNK])
        a_ref[:, lo:lo + FF_CHUNK] = (gate * jax.nn.sigmoid(gate) * up).astype(BF16)
    y = x + 0.5 * _dot(a_ref[...], wd_ref[...])
    if final:
        y = _rms(y, fn_ref[...])
    o_ref[...] = y


def _ffn(h, layer, norm_g, wgu, wd, final_g=None):
    t = h.shape[0]
    final = final_g is not None
    row = pl.BlockSpec((TM, D_MODEL), lambda i: (i, 0))
    in_specs = [row, _resident((1, D_MODEL), layer), _resident((D_MODEL, 2 * D_FF), layer),
                _resident((D_FF, D_MODEL), layer)]
    args = [h, norm_g, wgu, wd]
    if final:
        in_specs.append(_resident((1, D_MODEL)))
        args.append(final_g)
    return pl.pallas_call(
        functools.partial(_ffn_kernel, final=final),
        grid=(t // TM,),
        in_specs=in_specs,
        out_specs=row,
        out_shape=jax.ShapeDtypeStruct((t, D_MODEL), F32),
        scratch_shapes=[pltpu.VMEM((TM, D_FF), BF16)],
        compiler_params=_params("parallel"),
        name="ffn",
    )(*args)


LOG2E = math.log2(math.e)
Q_SCALE = DIFF_QK_DIM ** -0.5 * LOG2E
SQRT_HALF = math.sqrt(0.5)


def _gelu(x):
    return 0.5 * x * (1.0 + lax.erf(x * SQRT_HALF))


def _mix_kernel(h_ref, g_ref, win_ref, cos_ref, sa_ref, sb_ref, lng_ref, lnb_ref,
                ws_ref, bias_ref, qkv_ref, gm_ref):
    hn = _rms(h_ref[...], g_ref[...]).astype(BF16)
    cos = cos_ref[...]
    sa = sa_ref[...]
    sb = sb_ref[...]

    for part, scale in ((0, Q_SCALE), (1, 1.0)):
        proj = _dot(hn, win_ref[:, part * D_DIFF:(part + 1) * D_DIFF])
        for hd in range(N_DIFF_HEADS):
            t = proj[:, hd * LANES:(hd + 1) * LANES]
            r = (t * cos + pltpu.roll(t, LANES - ROPE_DIM // 2, 1) * sa
                 + pltpu.roll(t, ROPE_DIM // 2, 1) * sb)
            lo = part * D_DIFF + hd * LANES
            qkv_ref[:, lo:lo + LANES] = (r * scale).astype(BF16)
    qkv_ref[:, 2 * D_DIFF:] = _dot(hn, win_ref[:, 2 * D_DIFF:3 * D_DIFF]).astype(BF16)

    gu = _gelu(_dot(hn, win_ref[:, 3 * D_DIFF:3 * D_DIFF + D_GMLP]))
    gv = _gelu(_dot(hn, win_ref[:, 3 * D_DIFF + D_GMLP:]))
    tril = (lax.broadcasted_iota(jnp.int32, (CHUNK, CHUNK), 1)
            <= lax.broadcasted_iota(jnp.int32, (CHUNK, CHUNK), 0))
    for g in range(N_GMLP_GROUPS):
        cols = slice(g * GMLP_GROUP_DIM, (g + 1) * GMLP_GROUP_DIM)
        v = gv[:, cols]
        xc = v - jnp.mean(v, axis=-1, keepdims=True)
        vn = xc * lax.rsqrt(jnp.mean(xc * xc, axis=-1, keepdims=True) + NORM_EPS)
        vn = (vn * lng_ref[:, cols] + lnb_ref[:, cols]).astype(BF16)
        w = jnp.where(tril, ws_ref[g], 0.0).astype(BF16)
        for c in range(TM // CHUNK):
            rows = slice(c * CHUNK, (c + 1) * CHUNK)
            mixed = _dot(w, vn[rows, :]) + bias_ref[g]
            gm_ref[rows, cols] = (gu[rows, cols] * mixed).astype(BF16)


def _mix(h, layer, norm_g, w_in, cos, sa, sb, ln_g, ln_b, ws, bias):
    t = h.shape[0]
    tab = pl.BlockSpec((TM, LANES), lambda i: (i, 0))
    return pl.pallas_call(
        _mix_kernel,
        grid=(t // TM,),
        in_specs=[pl.BlockSpec((TM, D_MODEL), lambda i: (i, 0)),
                  _resident((1, D_MODEL), layer), _resident((D_MODEL, D_IN), layer),
                  tab, tab, tab,
                  _resident((1, D_GMLP), layer), _resident((1, D_GMLP), layer),
                  _resident((N_GMLP_GROUPS, CHUNK, CHUNK), layer),
                  _resident((N_GMLP_GROUPS, CHUNK, GMLP_GROUP_DIM), layer)],
        out_specs=[pl.BlockSpec((TM, 3 * D_DIFF), lambda i: (i, 0)),
                   pl.BlockSpec((TM, D_GMLP), lambda i: (i, 0))],
        out_shape=[jax.ShapeDtypeStruct((t, 3 * D_DIFF), BF16),
                   jax.ShapeDtypeStruct((t, D_GMLP), BF16)],
        compiler_params=_params("parallel"),
        name="mix_proj",
    )(h, norm_g, w_in, cos, sa, sb, ln_g, ln_b, ws, bias)


V_EXT = 2 * DIFF_V_DIM


def _attn_kernel(q_ref, k_ref, v_ref, lam_ref, sub_ref, o_ref,
                 vext_ref, qq_ref, sa_ref, sb_ref, mca_ref, mcb_ref, alpha_ref, m_ref, a_ref,
                 *, lam_init):
    qi = pl.program_id(2)

    @pl.when(qi == 0)
    def _():
        vext_ref[:, :DIFF_V_DIM] = v_ref[0]
        vext_ref[:, DIFF_V_DIM:] = jnp.ones((v_ref.shape[1], DIFF_V_DIM), BF16)

    q = q_ref[0]
    lane = lax.broadcasted_iota(jnp.int32, q.shape, 1)
    zero = jnp.zeros_like(q)
    qq_ref[:TQ, :] = jnp.where(lane < DIFF_QK_DIM, q, zero)
    qq_ref[TQ:, :] = jnp.where(lane >= DIFF_QK_DIM, q, zero)

    halves = (slice(0, TQ), slice(TQ, 2 * TQ))

    def block_off(j):
        return pl.multiple_of(j * TQ, TQ)

    def scores(j, buf, diagonal=False):
        s_ref, mc_ref = buf
        kb = k_ref[0, pl.ds(block_off(j), TQ), :]
        for rows in halves:
            s = lax.dot_general(qq_ref[rows, :], kb, NT_DIMS, preferred_element_type=F32)
            if diagonal:
                causal = (lax.broadcasted_iota(jnp.int32, (TQ, TQ), 1)
                          <= lax.broadcasted_iota(jnp.int32, (TQ, TQ), 0))
                s = jnp.where(causal, s, NEG_INF)
            s_ref[rows, :] = s
            cols = [s[:, c * LANES:(c + 1) * LANES] for c in range(TQ // LANES)]
            m_cur = jnp.max(functools.reduce(jnp.maximum, cols), axis=-1, keepdims=True)
            mc_ref[rows, :] = jnp.broadcast_to(m_cur, (TQ, LANES))

    def update(j, buf):
        s_ref, mc_ref = buf
        vb = vext_ref[pl.ds(block_off(j), TQ), :]
        probs = []
        for rows in halves:
            m_old = m_ref[rows, :]
            m_new = jnp.maximum(m_old, mc_ref[rows, :])
            alpha_ref[rows, :] = jnp.exp2(m_old - m_new)
            m_ref[rows, :] = m_new
            probs.append(jnp.concatenate(
                [jnp.exp2((s_ref[rows, c * LANES:(c + 1) * LANES] - m_new).astype(BF16))
                 for c in range(TQ // LANES)], axis=1))
        for rows, p in zip(halves, probs):
            pv = _dot(p, vb)
            alpha = alpha_ref[rows, :]
            a_ref[rows, :] = jnp.concatenate([alpha, alpha], axis=1) * a_ref[rows, :] + pv

    buf_a, buf_b = (sa_ref, mca_ref), (sb_ref, mcb_ref)
    m_ref[...] = jnp.full(m_ref.shape, NEG_INF, F32)
    a_ref[...] = jnp.zeros(a_ref.shape, F32)

    @pl.when(qi == 0)
    def _():
        scores(0, buf_a, diagonal=True)
        update(0, buf_a)

    @pl.when(qi > 0)
    def _():
        scores(0, buf_a)

    def pair(t, carry):
        j = 2 * t
        scores(j + 1, buf_b)
        update(j, buf_a)
        scores(j + 2, buf_a)
        update(j + 1, buf_b)
        return carry

    n_pairs = jnp.maximum(qi - 1, 0) // 2
    lax.fori_loop(0, n_pairs, pair, 0)

    @pl.when(qi % 2 == 1)
    def _():
        scores(qi, buf_b, diagonal=True)
        update(qi - 1, buf_a)
        update(qi, buf_b)

    @pl.when((qi % 2 == 0) & (qi > 0))
    def _():
        scores(qi - 1, buf_b)
        update(qi - 2, buf_a)
        scores(qi, buf_a, diagonal=True)
        update(qi - 1, buf_b)
        update(qi, buf_a)

    lv = lam_ref[...]
    lam = (jnp.exp(jnp.sum(lv[0:1] * lv[1:2], axis=-1, keepdims=True))
           - jnp.exp(jnp.sum(lv[2:3] * lv[3:4], axis=-1, keepdims=True)) + lam_init)
    o = (a_ref[:TQ, :DIFF_V_DIM] / a_ref[:TQ, DIFF_V_DIM:]
         - lam * (a_ref[TQ:, :DIFF_V_DIM] / a_ref[TQ:, DIFF_V_DIM:]))
    o_ref[0] = (_rms(o, sub_ref[...]) * (1.0 - lam_init)).astype(BF16)


def _diff_attention(qkv, layer, lam_vecs, subln, batch, seq):
    h = N_DIFF_HEADS
    lam_init = 0.8 - 0.6 * math.exp(-0.3 * layer)
    kv_spec = lambda part: pl.BlockSpec(
        (1, seq, LANES), lambda b, hd, qi: (b, 0, part * h + hd))
    return pl.pallas_call(
        functools.partial(_attn_kernel, lam_init=lam_init),
        grid=(batch, h, seq // TQ),
        in_specs=[pl.BlockSpec((1, TQ, LANES), lambda b, hd, qi: (b, qi, hd)),
                  kv_spec(1), kv_spec(2),
                  _resident((4, DIFF_QK_DIM), layer), _resident((1, DIFF_V_DIM), layer)],
        out_specs=pl.BlockSpec((1, TQ, LANES), lambda b, hd, qi: (b, qi, hd)),
        out_shape=jax.ShapeDtypeStruct((batch, seq, D_DIFF), BF16),
        scratch_shapes=[pltpu.VMEM((seq, V_EXT), BF16), pltpu.VMEM((2 * TQ, LANES), BF16),
                        pltpu.VMEM((2 * TQ, TQ), F32), pltpu.VMEM((2 * TQ, TQ), F32)]
        + [pltpu.VMEM((2 * TQ, LANES), F32)] * 4 + [pltpu.VMEM((2 * TQ, V_EXT), F32)],
        compiler_params=_params("parallel", "parallel", "arbitrary"),
        name="diff_attn",
    )(qkv, qkv, qkv, lam_vecs, subln)


def _memkv_kernel(mem_ref, g_ref, wkv_ref, k_ref, v_ref):
    mn = _rms(mem_ref[...], g_ref[...]).astype(BF16)
    kv = _dot(mn, wkv_ref[...])
    k_ref[...] = kv[:, :D_MODEL].astype(BF16)
    v_ref[...] = kv[:, D_MODEL:].astype(BF16)


def _mem_kv(mem2d, layer, norm_g, wkv):
    t = mem2d.shape[0]
    row = pl.BlockSpec((N_MEM, D_MODEL), lambda i: (i, 0))
    out = jax.ShapeDtypeStruct((t, D_MODEL), BF16)
    return pl.pallas_call(
        _memkv_kernel,
        grid=(t // N_MEM,),
        in_specs=[row, _resident((1, D_MODEL), layer),
                  _resident((D_MODEL, 2 * D_MODEL), layer)],
        out_specs=[row, row],
        out_shape=[out, out],
        compiler_params=_params("parallel"),
        name="mem_kv",
    )(mem2d, norm_g, wkv)


CROSS_SCALE = CROSS_HEAD_DIM ** -0.5


def _outcross_kernel(h_ref, attn_ref, gm_ref, woa_ref, wog_ref, cn_ref, wq_ref,
                     k_ref, v_ref, wo_ref, o_ref, ctx_ref):
    h2 = h_ref[...] + _dot(attn_ref[...], woa_ref[...]) + _dot(gm_ref[...], wog_ref[...])
    hn = _rms(h2, cn_ref[...]).astype(BF16)
    q = (_dot(hn, wq_ref[...]) * CROSS_SCALE).astype(BF16)
    for hd in range(N_CROSS_HEADS):
        cols = slice(hd * CROSS_HEAD_DIM, (hd + 1) * CROSS_HEAD_DIM)
        sc = lax.dot_general(q[:, cols], k_ref[:, cols], NT_DIMS, preferred_element_type=F32)
        p = jnp.exp(sc - jnp.max(sc, axis=-1, keepdims=True))
        p = p / jnp.sum(p, axis=-1, keepdims=True)
        ctx_ref[:, cols] = _dot(p.astype(BF16), v_ref[:, cols]).astype(BF16)
    o_ref[...] = h2 + _dot(ctx_ref[...], wo_ref[...])


def _out_cross(h, layer, attn, gm, w_out, norm_g, wq, kmem, vmem, wo, seq):
    t = h.shape[0]
    row = pl.BlockSpec((TM, D_MODEL), lambda i: (i, 0))
    half = pl.BlockSpec((TM, D_DIFF), lambda i: (i, 0))
    mem = pl.BlockSpec((N_MEM, D_MODEL), lambda i: (i * TM // seq, 0))
    return pl.pallas_call(
        _outcross_kernel,
        grid=(t // TM,),
        in_specs=[row, half, half,
                  _resident((D_DIFF, D_MODEL), layer, 0), _resident((D_GMLP, D_MODEL), layer, 1),
                  _resident((1, D_MODEL), layer), _resident((D_MODEL, D_MODEL), layer),
                  mem, mem, _resident((D_MODEL, D_MODEL), layer)],
        out_specs=row,
        out_shape=jax.ShapeDtypeStruct((t, D_MODEL), F32),
        scratch_shapes=[pltpu.VMEM((TM, D_MODEL), BF16)],
        compiler_params=_params("parallel"),
        name="out_cross",
    )(h, attn, gm, w_out, w_out, norm_g, wq, kmem, vmem, wo)


def kernel(x, mem, positions, ffn1_norm, ffn1_wgu, ffn1_wd, mix_norm, w_in, lam_q1, lam_k1,
           lam_q2, lam_k2, subln, gmlp_ln_g, gmlp_ln_b, gmlp_ws, gmlp_bs, w_out, cross_norm,
           mem_norm, cross_wq, cross_wkv, cross_wo, ffn2_norm, ffn2_wgu, ffn2_wd, final_norm):
    batch, seq, _ = x.shape
    t = batch * seq
    assert seq % TQ == 0 and seq % TM == 0 and t % ROPE_TM == 0
    cos, sa, sb = _rope_tables(positions)
    h = x.reshape(t, D_MODEL)
    mem2d = mem.reshape(batch * N_MEM, D_MODEL)

    bf = lambda w: w.astype(BF16)
    vec = lambda g: g.reshape(DEPTH, 1, -1)
    wgu1, wd1, wgu2, wd2 = bf(ffn1_wgu), bf(ffn1_wd), bf(ffn2_wgu), bf(ffn2_wd)
    w_in_b, w_out_b = bf(w_in), bf(w_out).reshape(DEPTH, 2, D_DIFF, D_MODEL)
    wq_b, wkv_b, wo_b = bf(cross_wq), bf(cross_wkv), bf(cross_wo)
    lam_vecs = jnp.stack([lam_q1, lam_k1, lam_q2, lam_k2], axis=1)
    bias = jnp.broadcast_to(gmlp_bs[..., None], gmlp_bs.shape + (GMLP_GROUP_DIM,))

    for i in range(DEPTH):
        h = _ffn(h, i, vec(ffn1_norm), wgu1, wd1)
        qkv, gm = _mix(h, i, vec(mix_norm), w_in_b, cos, sa, sb, vec(gmlp_ln_g), vec(gmlp_ln_b),
                       gmlp_ws, bias)
        attn = _diff_attention(qkv.reshape(batch, seq, 3 * D_DIFF), i, lam_vecs, vec(subln),
                               batch, seq)
        kmem, vmem = _mem_kv(mem2d, i, vec(mem_norm), wkv_b)
        h = _out_cross(h, i, attn.reshape(t, D_DIFF), gm, w_out_b, vec(cross_norm), wq_b,
                       kmem, vmem, wo_b, seq)
        h = _ffn(h, i, vec(ffn2_norm), wgu2, wd2,
                 final_g=final_norm[None, :] if i == DEPTH - 1 else None)
    return h.reshape(batch, seq, D_MODEL)
```
